```python
import math
import jax, jax.numpy as jnp
from jax import lax
import numpy as np

D_MODEL = 1024
BATCH = 32
SEQ = 256
DEPTH = 1
DEC_BATCH = 4
DEC_SEQ = 1024
PAST_LEN = 512

GRID_W = 64
CHUNK = 128
N_RET_HEADS = 4
RET_DK = D_MODEL // 8
RET_DV = D_MODEL // 4
RET_QK = N_RET_HEADS * RET_DK
RET_V = N_RET_HEADS * RET_DV
N_SG_GROUPS = 4
SG_WIDTH = D_MODEL
SG_GROUP = SG_WIDTH // N_SG_GROUPS
D_FF = ((8 * D_MODEL // 3 + 127) // 128) * 128
N_MOD = 9
ROPE_THETA = 10000.0
EPS = 1e-6
IN_SIZES = (RET_QK, RET_QK, RET_V, RET_V, SG_WIDTH, SG_WIDTH, 2 * D_MODEL)
IN_WIDTH = sum(IN_SIZES)
IN_SPLITS = tuple(int(s) for s in np.cumsum(IN_SIZES)[:-1])

kernel_name = "hybrid_retention_gmlp_diffusion_step"


def _rms(x, g):
    xf = x.astype(jnp.float32)
    y = xf * lax.rsqrt(jnp.mean(xf * xf, axis=-1, keepdims=True) + EPS)
    return (y * g.astype(jnp.float32)).astype(x.dtype)


def _swiglu(h, w1, w2):
    a, b = jnp.split(h @ w1, 2, axis=-1)
    return (jax.nn.silu(a) * b) @ w2


def _axial_rope(L):
    rows = L // GRID_W
    r = jnp.repeat(jnp.arange(rows), GRID_W).astype(jnp.float32)
    col = (jnp.arange(rows * GRID_W) % GRID_W).astype(jnp.float32)
    nf = RET_DK // 4
    freqs = ROPE_THETA ** (-jnp.arange(nf, dtype=jnp.float32) / nf)
    ang = jnp.concatenate([r[:, None] * freqs, col[:, None] * freqs], axis=-1)
    return jnp.cos(ang), jnp.sin(ang)


def _apply_rope(x, cos, sin):
    half = RET_DK // 2
    x1, x2 = x[..., :half], x[..., half:]
    return jnp.concatenate([x1 * cos - x2 * sin, x1 * sin + x2 * cos], axis=-1)


def _retention_chunkwise(q, k, v, log_g, S0):
    B, H, L, _ = q.shape
    n = L // CHUNK
    idx = jnp.arange(CHUNK, dtype=jnp.float32)
    diff = idx[:, None] - idx[None, :]
    causal = diff >= 0
    dmask = jnp.where(causal, jnp.exp(log_g[:, None, None] * jnp.where(causal, diff, 0.0)), 0.0)
    q_dec = jnp.exp(log_g[:, None] * (idx + 1.0))[..., None]
    k_dec = jnp.exp(log_g[:, None] * (CHUNK - 1.0 - idx))[..., None]
    chunk_dec = jnp.exp(log_g * CHUNK)[:, None, None]

    def to_chunks(t):
        return t.reshape(B, H, n, CHUNK, t.shape[-1]).transpose(2, 0, 1, 3, 4)

    def step(S, qkv):
        qc, kc, vc = qkv
        scores = jnp.einsum('bhid,bhjd->bhij', qc, kc) * dmask
        o = (jnp.einsum('bhij,bhje->bhie', scores, vc)
             + jnp.einsum('bhid,bhde->bhie', qc * q_dec, S))
        S = S * chunk_dec + jnp.einsum('bhjd,bhje->bhde', kc * k_dec, vc)
        return S, o

    S, o = lax.scan(step, S0, (to_chunks(q), to_chunks(k), to_chunks(v)))
    o = o.transpose(1, 2, 0, 3, 4).reshape(B, H, L, v.shape[-1])
    return o, S


def _mixer(h, rope, S0f, S0b, w_in, decay_logit, g_ret, w_ret_br, g_sg, b_sg, w_sp, b_sp, w_sg_br, w_out):
    B, L, _ = h.shape
    z = h @ w_in
    q, k, v, g, u, vs, gates = jnp.split(z, IN_SPLITS, axis=-1)

    def heads(t, d):
        return t.reshape(B, L, N_RET_HEADS, d).transpose(0, 2, 1, 3).astype(jnp.float32)
    qh = heads(q, RET_DK) * (RET_DK ** -0.5)
    kh = heads(k, RET_DK)
    vh = heads(v, RET_DV)
    if rope is not None:
        qh = _apply_rope(qh, *rope)
        kh = _apply_rope(kh, *rope)
    log_g = jax.nn.log_sigmoid(decay_logit.astype(jnp.float32))
    of, Sf = _retention_chunkwise(qh, kh, vh, log_g[0], S0f.astype(jnp.float32))
    flip = lambda t: jnp.flip(t, axis=2)
    ob, Sb = _retention_chunkwise(flip(qh), flip(kh), flip(vh), log_g[1], S0b.astype(jnp.float32))
    r = of + flip(ob)
    mu = jnp.mean(r, axis=-1, keepdims=True)
    rc = r - mu
    r = rc * lax.rsqrt(jnp.mean(rc * rc, axis=-1, keepdims=True) + EPS)
    r = r.transpose(0, 2, 1, 3).reshape(B, L, RET_V) * g_ret.astype(jnp.float32)
    ret_out = (jax.nn.silu(g.astype(jnp.float32)) * r).astype(h.dtype) @ w_ret_br

    u = jax.nn.gelu(u)
    vsf = jax.nn.gelu(vs).astype(jnp.float32)
    vm = jnp.mean(vsf, axis=-1, keepdims=True)
    vc_ = vsf - vm
    vsn = vc_ * lax.rsqrt(jnp.mean(vc_ * vc_, axis=-1, keepdims=True) + EPS)
    vsn = (vsn * g_sg.astype(jnp.float32) + b_sg.astype(jnp.float32)).astype(h.dtype)
    n = L // CHUNK
    vsn = vsn.reshape(B, n, CHUNK, N_SG_GROUPS, SG_GROUP)
    sp = jnp.einsum('gij,bnjgc->bnigc', w_sp, vsn) + b_sp.T[:, :, None]
    sg_out = (u * sp.reshape(B, L, SG_WIDTH)) @ w_sg_br

    gate_r, gate_s = jnp.split(jax.nn.sigmoid(gates), 2, axis=-1)
    y = (gate_r * ret_out + gate_s * sg_out) @ w_out
    return y, Sf, Sb


def _layer(x, mod, rope, S0f, S0b, g_norm, w_ffn1_in, w_ffn1_out, w_in, decay_logit, g_ret, w_ret_br,
           g_sg, b_sg, w_sp, b_sp, w_sg_br, w_out, w_ffn2_in, w_ffn2_out):
    mo = lambda i: mod[:, i][:, None, :]
    hh = _rms(x, g_norm[0]) * (1 + mo(1)) + mo(0)
    x = x + 0.5 * mo(2) * _rms(_swiglu(hh, w_ffn1_in, w_ffn1_out), g_norm[1])
    hh = _rms(x, g_norm[2]) * (1 + mo(4)) + mo(3)
    y, Sf, Sb = _mixer(hh, rope, S0f, S0b, w_in, decay_logit, g_ret, w_ret_br, g_sg, b_sg, w_sp, b_sp, w_sg_br, w_out)
    x = x + mo(5) * _rms(y, g_norm[3])
    hh = _rms(x, g_norm[4]) * (1 + mo(7)) + mo(6)
    x = x + 0.5 * mo(8) * _rms(_swiglu(hh, w_ffn2_in, w_ffn2_out), g_norm[5])
    return x, Sf, Sb


def setup_inputs(seed: int = 0) -> dict:
    key = jax.random.key(seed)
    ks = jax.random.split(key, 24)
    nrm = lambda k, s, sc=1.0: jax.random.normal(k, s, jnp.float32) * sc
    D = D_MODEL
    base_logit = jnp.log(2.0 ** (5.0 + jnp.arange(N_RET_HEADS, dtype=jnp.float32)) - 1.0)
    return {
        "x_prompt": nrm(ks[0], (BATCH, SEQ, D)),
        "x_sample": nrm(ks[1], (DEC_BATCH, DEC_SEQ, D)),
        "state_ret_fwd": nrm(ks[2], (DEC_BATCH, DEPTH, N_RET_HEADS, RET_DK, RET_DV), 0.3),
        "state_ret_bwd": nrm(ks[3], (DEC_BATCH, DEPTH, N_RET_HEADS, RET_DK, RET_DV), 0.3),
        "c": nrm(ks[4], (DEC_BATCH, D)),
        "c_ctx": nrm(ks[5], (D,)),
        "w_ada": nrm(ks[6], (DEPTH, D, N_MOD * D), 0.3 * D ** -0.5),
        "b_ada": nrm(ks[7], (DEPTH, N_MOD * D), 0.1),
        "g_norm": 1.0 + nrm(ks[8], (DEPTH, 6, D), 0.1),
        "w_ffn1_in": nrm(ks[9], (DEPTH, D, 2 * D_FF), D ** -0.5),
        "w_ffn1_out": nrm(ks[10], (DEPTH, D_FF, D), D_FF ** -0.5),
        "w_in": nrm(ks[11], (DEPTH, D, IN_WIDTH), D ** -0.5),
        "ret_decay_logit": base_logit[None, None, :] + nrm(ks[12], (DEPTH, 2, N_RET_HEADS), 0.1),
        "g_ret": 1.0 + nrm(ks[13], (DEPTH, RET_V), 0.1),
        "w_ret_br": nrm(ks[14], (DEPTH, RET_V, D), RET_V ** -0.5),
        "g_sg": 1.0 + nrm(ks[15], (DEPTH, SG_WIDTH), 0.1),
        "b_sg": nrm(ks[16], (DEPTH, SG_WIDTH), 0.02),
        "w_sp": nrm(ks[17], (DEPTH, N_SG_GROUPS, CHUNK, CHUNK), CHUNK ** -0.5),
        "b_sp": 1.0 + nrm(ks[18], (DEPTH, N_SG_GROUPS, CHUNK), 0.1),
        "w_sg_br": nrm(ks[19], (DEPTH, SG_WIDTH, D), SG_WIDTH ** -0.5),
        "w_out": nrm(ks[20], (DEPTH, D, D), D ** -0.5),
        "w_ffn2_in": nrm(ks[21], (DEPTH, D, 2 * D_FF), D ** -0.5),
        "w_ffn2_out": nrm(ks[22], (DEPTH, D_FF, D), D_FF ** -0.5),
    }


def reference(x_prompt, x_sample, state_ret_fwd, state_ret_bwd, c, c_ctx, w_ada, b_ada, g_norm,
              w_ffn1_in, w_ffn1_out, w_in, ret_decay_logit, g_ret, w_ret_br, g_sg, b_sg, w_sp, b_sp,
              w_sg_br, w_out, w_ffn2_in, w_ffn2_out):
    yp = x_prompt
    ys = x_sample
    Bp = x_prompt.shape[0]
    Bs = x_sample.shape[0]
    rope = _axial_rope(x_sample.shape[1])
    zero_state = jnp.zeros((Bp, N_RET_HEADS, RET_DK, RET_DV), jnp.float32)
    new_f, new_b = [], []
    for l in range(DEPTH):
        lw = (g_norm[l], w_ffn1_in[l], w_ffn1_out[l], w_in[l], ret_decay_logit[l], g_ret[l], w_ret_br[l],
              g_sg[l], b_sg[l], w_sp[l], b_sp[l], w_sg_br[l], w_out[l], w_ffn2_in[l], w_ffn2_out[l])
        mod_ctx = (jax.nn.silu(c_ctx)[None, :] @ w_ada[l] + b_ada[l]).reshape(1, N_MOD, D_MODEL)
        mod_lat = (jax.nn.silu(c) @ w_ada[l] + b_ada[l]).reshape(Bs, N_MOD, D_MODEL)
        yp, Sf, Sb = _layer(yp, mod_ctx, None, zero_state, zero_state, *lw)
        new_f.append(Sf)
        new_b.append(Sb)
        ys, _, _ = _layer(ys, mod_lat, rope, state_ret_fwd[:, l], state_ret_bwd[:, l], *lw)
    new_state_ret_fwd = jnp.stack(new_f, axis=1).astype(x_prompt.dtype)
    new_state_ret_bwd = jnp.stack(new_b, axis=1).astype(x_prompt.dtype)
    return (yp, ys, new_state_ret_fwd, new_state_ret_bwd)
```

```python
import functools
import math

import jax
import jax.numpy as jnp
from jax import lax
from jax.experimental import pallas as pl
from jax.experimental.pallas import tpu as pltpu

D_MODEL = 1024
GRID_W = 64
CHUNK = 128
N_HEADS = 4
DK = D_MODEL // 8
DV = D_MODEL // 4
N_GROUPS = 4
GROUP_W = D_MODEL // N_GROUPS
D_FF = 2816
FF_CHUNK = 256
N_FF_CHUNKS = D_FF // FF_CHUNK
N_MOD = 9
N_IN_SEG = 7
ROPE_THETA = 10000.0
EPS = 1e-6
TOKEN_TILE = 512
VMEM_LIMIT_BYTES = 56 * 1024 * 1024

F32 = jnp.float32
BF16 = jnp.bfloat16


def _params(n_axes=1):
    return pltpu.CompilerParams(
        dimension_semantics=("parallel",) * n_axes, vmem_limit_bytes=VMEM_LIMIT_BYTES)


def _resident(shape):
    nd = len(shape)
    return pl.BlockSpec(shape, lambda *_: (0,) * nd, pipeline_mode=pl.Buffered(1))


def _rms(x, g):
    return x * lax.rsqrt(jnp.mean(x * x, axis=-1, keepdims=True) + EPS) * g


def _dot(a, b):
    return jnp.dot(a, b, preferred_element_type=F32)


def _mod_kernel(c_ref, w_ref, b_ref, o_ref):
    c = c_ref[...]
    a = (c * jax.nn.sigmoid(c)).astype(BF16)
    o_ref[...] = _dot(a, w_ref[...].astype(BF16)) + b_ref[...]


def _modulation(c_rows, w_ada, b_ada):
    n = w_ada.shape[1]
    tn = D_MODEL
    return pl.pallas_call(
        _mod_kernel,
        grid=(n // tn,),
        in_specs=[pl.BlockSpec((16, D_MODEL), lambda j: (0, 0)),
                  pl.BlockSpec((D_MODEL, tn), lambda j: (0, j)),
                  pl.BlockSpec((1, tn), lambda j: (0, j))],
        out_specs=pl.BlockSpec((16, tn), lambda j: (0, j)),
        out_shape=jax.ShapeDtypeStruct((16, n), F32),
        compiler_params=_params(),
        name="adaln_mod",
    )(c_rows, w_ada, b_ada.reshape(1, n))


def _ffn_kernel(x_ref, mod_ref, g_ref, w1_ref, w2_ref, o_ref, *, mod0, gn0):
    x = x_ref[...]
    mod = mod_ref[0]
    g = g_ref[...]
    hh = (_rms(x, g[gn0:gn0 + 1]) * (1.0 + mod[mod0 + 1:mod0 + 2]) + mod[mod0:mod0 + 1]).astype(BF16)
    acc = None
    for j in range(N_FF_CHUNKS):
        a = _dot(hh, w1_ref[0, j])
        b = _dot(hh, w1_ref[1, j])
        act = (a * jax.nn.sigmoid(a) * b).astype(BF16)
        y = _dot(act, w2_ref[j])
        acc = y if acc is None else acc + y
    o_ref[...] = x + 0.5 * mod[mod0 + 2:mod0 + 3] * _rms(acc, g[gn0 + 1:gn0 + 2])


def _ffn(x, mod, g_norm, w1, w2, *, mod0, gn0, rows_per_mod):
    t = x.shape[0]
    tm = TOKEN_TILE
    if rows_per_mod is None:
        mod_map = lambda i: (0, 0, 0)
    else:
        mod_map = lambda i: ((i * tm) // rows_per_mod, 0, 0)
    return pl.pallas_call(
        functools.partial(_ffn_kernel, mod0=mod0, gn0=gn0),
        grid=(t // tm,),
        in_specs=[pl.BlockSpec((tm, D_MODEL), lambda i: (i, 0)),
                  pl.BlockSpec((1, N_MOD, D_MODEL), mod_map),
                  _resident(g_norm.shape),
                  _resident(w1.shape),
                  _resident(w2.shape)],
        out_specs=pl.BlockSpec((tm, D_MODEL), lambda i: (i, 0)),
        out_shape=jax.ShapeDtypeStruct((t, D_MODEL), F32),
        compiler_params=_params(),
        name="macaron_ffn",
    )(x, mod, g_norm, w1, w2)


def _rope_slab(x, cos, sin_signed):
    return x * cos + pltpu.roll(x, DK // 2, 1) * sin_signed


def _mixer_in_kernel(*refs, use_rope):
    if use_rope:
        (x_ref, mod_ref, g_ref, w_ref, gsg_ref, bsg_ref, cos_ref, sin_ref,
         q_ref, k_ref, v_ref, sg_ref, ug_ref, vsn_ref, gr_ref, gs_ref) = refs
    else:
        (x_ref, mod_ref, g_ref, w_ref, gsg_ref, bsg_ref,
         q_ref, k_ref, v_ref, sg_ref, ug_ref, vsn_ref, gr_ref, gs_ref) = refs
    x = x_ref[...]
    mod = mod_ref[0]
    g = g_ref[...]
    hh = (_rms(x, g[2:3]) * (1.0 + mod[4:5]) + mod[3:4]).astype(BF16)

    z = _dot(hh, w_ref[0])
    q = z[:, :N_HEADS * DK] * (DK ** -0.5)
    k = z[:, N_HEADS * DK:]
    if use_rope:
        cos = cos_ref[...]
        sin = sin_ref[...]
        for h in range(N_HEADS):
            sl = slice(h * DK, (h + 1) * DK)
            q_ref[:, sl] = _rope_slab(q[:, sl], cos, sin).astype(BF16)
            k_ref[:, sl] = _rope_slab(k[:, sl], cos, sin).astype(BF16)
    else:
        q_ref[...] = q.astype(BF16)
        k_ref[...] = k.astype(BF16)

    v_ref[...] = _dot(hh, w_ref[1]).astype(BF16)

    z = _dot(hh, w_ref[2])
    sg_ref[...] = (z * jax.nn.sigmoid(z)).astype(BF16)

    ug_ref[...] = jax.nn.gelu(_dot(hh, w_ref[3])).astype(BF16)

    vs = jax.nn.gelu(_dot(hh, w_ref[4]))
    vc = vs - jnp.mean(vs, axis=-1, keepdims=True)
    vsn = vc * lax.rsqrt(jnp.mean(vc * vc, axis=-1, keepdims=True) + EPS)
    vsn_ref[...] = (vsn * gsg_ref[...] + bsg_ref[...]).astype(BF16)

    gr_ref[...] = jax.nn.sigmoid(_dot(hh, w_ref[5])).astype(BF16)
    gs_ref[...] = jax.nn.sigmoid(_dot(hh, w_ref[6])).astype(BF16)


def _mixer_in(x, mod, g_norm, w_in, g_sg, b_sg, rope, *, rows_per_mod, seq_len):
    t = x.shape[0]
    tm = TOKEN_TILE
    if rows_per_mod is None:
        mod_map = lambda i: (0, 0, 0)
    else:
        mod_map = lambda i: ((i * tm) // rows_per_mod, 0, 0)
    tok = lambda w: pl.BlockSpec((tm, w), lambda i: (i, 0))
    in_specs = [tok(D_MODEL), pl.BlockSpec((1, N_MOD, D_MODEL), mod_map), _resident(g_norm.shape),
                _resident(w_in.shape), _resident(g_sg.shape), _resident(b_sg.shape)]
    args = [x, mod, g_norm, w_in, g_sg, b_sg]
    if rope is not None:
        tiles_per_seq = seq_len // tm
        rope_spec = pl.BlockSpec((tm, DK), lambda i: (i % tiles_per_seq, 0))
        in_specs += [rope_spec, rope_spec]
        args += list(rope)
    widths = (N_HEADS * DK, N_HEADS * DK) + (D_MODEL,) * 6
    return pl.pallas_call(
        functools.partial(_mixer_in_kernel, use_rope=rope is not None),
        grid=(t // tm,),
        in_specs=in_specs,
        out_specs=[tok(w) for w in widths],
        out_shape=[jax.ShapeDtypeStruct((t, w), BF16) for w in widths],
        compiler_params=_params(),
        name="mixer_in",
    )(*args)


def _log_decay(logit_ref, d, h):
    x = logit_ref[d:d + 1, h:h + 1]
    return jnp.minimum(x, 0.0) - jnp.log1p(jnp.exp(-jnp.abs(x)))


def _state_kernel(*refs, n_chunks, zero_init):
    if zero_init:
        logit_ref, k_ref, v_ref, sf_ref, sb_ref, finf_ref, finb_ref = refs
    else:
        logit_ref, k_ref, v_ref, s0f_ref, s0b_ref, sf_ref, sb_ref = refs
    row = lax.broadcasted_iota(jnp.int32, (CHUNK, DK), 0).astype(F32)
    for h in range(N_HEADS):
        lgf = _log_decay(logit_ref, 0, h)
        lgb = _log_decay(logit_ref, 1, h)
        kdec_f = jnp.exp(lgf * (CHUNK - 1.0 - row))
        kdec_b = jnp.exp(lgb * row)
        gc_f = jnp.exp(lgf * CHUNK)
        gc_b = jnp.exp(lgb * CHUNK)

        def kv(c, kdec):
            rows = slice(c * CHUNK, (c + 1) * CHUNK)
            kd = (k_ref[rows, h * DK:(h + 1) * DK].astype(F32) * kdec).astype(BF16)
            vc = v_ref[rows, h * DV:(h + 1) * DV]
            return lax.dot_general(kd, vc, (((0,), (0,)), ((), ())), preferred_element_type=F32)

        s = jnp.zeros((DK, DV), F32) if zero_init else s0f_ref[0, h]
        for c in range(n_chunks):
            sf_ref[c, h] = s.astype(BF16)
            s = s * gc_f + kv(c, kdec_f)
        if zero_init:
            finf_ref[0, h] = s

        s = jnp.zeros((DK, DV), F32) if zero_init else s0b_ref[0, h]
        for c in range(n_chunks - 1, -1, -1):
            sb_ref[c, h] = s.astype(BF16)
            s = s * gc_b + kv(c, kdec_b)
        if zero_init:
            finb_ref[0, h] = s


def _retention_states(decay_logit, k, v, s0, *, seq_len):
    t = k.shape[0]
    n_seq = t // seq_len
    n_chunks = seq_len // CHUNK
    zero_init = s0 is None
    state_spec = pl.BlockSpec((n_chunks, N_HEADS, DK, DV), lambda s: (s, 0, 0, 0))
    per_seq = pl.BlockSpec((1, N_HEADS, DK, DV), lambda s: (s, 0, 0, 0))
    in_specs = [_resident(decay_logit.shape),
                pl.BlockSpec((seq_len, N_HEADS * DK), lambda s: (s, 0)),
                pl.BlockSpec((seq_len, D_MODEL), lambda s: (s, 0))]
    args = [decay_logit, k, v]
    states = jax.ShapeDtypeStruct((t // CHUNK, N_HEADS, DK, DV), BF16)
    out_specs = [state_spec, state_spec]
    out_shape = [states, states]
    if zero_init:
        final = jax.ShapeDtypeStruct((n_seq, N_HEADS, DK, DV), F32)
        out_specs += [per_seq, per_seq]
        out_shape += [final, final]
    else:
        in_specs += [per_seq, per_seq]
        args += list(s0)
    return pl.pallas_call(
        functools.partial(_state_kernel, n_chunks=n_chunks, zero_init=zero_init),
        grid=(n_seq,),
        in_specs=in_specs,
        out_specs=out_specs,
        out_shape=out_shape,
        compiler_params=_params(),
        name="retention_states",
    )(*args)


def _mixer_core_kernel(logit_ref, x_ref, mod_ref, g_ref, q_ref, k_ref, v_ref, sg_ref, ug_ref, vsn_ref,
                       gr_ref, gs_ref, sf_ref, sb_ref, gret_ref, wsp_ref, bsp_ref, wret_ref, wsg_ref,
                       wout_ref, o_ref, rg_ref, sgin_ref, *, chunks_per_seq, zero_init):
    n_tile_chunks = x_ref.shape[0] // CHUNK
    row = lax.broadcasted_iota(jnp.int32, (CHUNK, CHUNK), 0).astype(F32)
    col = lax.broadcasted_iota(jnp.int32, (CHUNK, CHUNK), 1).astype(F32)
    diff = row - col

    for h in range(N_HEADS):
        lgf = _log_decay(logit_ref, 0, h)
        lgb = _log_decay(logit_ref, 1, h)
        dmask = (jnp.where(diff >= 0, jnp.exp(lgf * jnp.maximum(diff, 0.0)), 0.0)
                 + jnp.where(diff <= 0, jnp.exp(lgb * jnp.maximum(-diff, 0.0)), 0.0))
        qdec_f = jnp.exp(lgf * (row + 1.0))
        qdec_b = jnp.exp(lgb * (CHUNK - row))
        gret = gret_ref[:, h * DV:(h + 1) * DV]
        for c in range(n_tile_chunks):
            rows = slice(c * CHUNK, (c + 1) * CHUNK)
            qc = q_ref[rows, h * DK:(h + 1) * DK]
            kc = k_ref[rows, h * DK:(h + 1) * DK]
            vc = v_ref[rows, h * DV:(h + 1) * DV]
            scores = lax.dot_general(qc, kc, (((1,), (1,)), ((), ())), preferred_element_type=F32)
            o = _dot((scores * dmask).astype(BF16), vc)
            qf = qc.astype(F32)
            seq_pos = c % chunks_per_seq
            if not (zero_init and seq_pos == 0):
                o = o + _dot((qf * qdec_f).astype(BF16), sf_ref[c, h])
            if not (zero_init and seq_pos == chunks_per_seq - 1):
                o = o + _dot((qf * qdec_b).astype(BF16), sb_ref[c, h])
            oc = o - jnp.mean(o, axis=-1, keepdims=True)
            rn = oc * lax.rsqrt(jnp.mean(oc * oc, axis=-1, keepdims=True) + EPS)
            sg = sg_ref[rows, h * DV:(h + 1) * DV].astype(F32)
            rg_ref[rows, h * DV:(h + 1) * DV] = (sg * (rn * gret)).astype(BF16)

    for gi in range(N_GROUPS):
        wsp = wsp_ref[gi]
        bias = bsp_ref[:, gi:gi + 1]
        cols = slice(gi * GROUP_W, (gi + 1) * GROUP_W)
        for c in range(n_tile_chunks):
            rows = slice(c * CHUNK, (c + 1) * CHUNK)
            sp = _dot(wsp, vsn_ref[rows, cols]) + bias
            sgin_ref[rows, cols] = (ug_ref[rows, cols].astype(F32) * sp).astype(BF16)

    ret_out = _dot(rg_ref[...], wret_ref[...])
    sg_out = _dot(sgin_ref[...], wsg_ref[...])
    merged = gr_ref[...].astype(F32) * ret_out + gs_ref[...].astype(F32) * sg_out
    y = _dot(merged.astype(BF16), wout_ref[...])
    mod = mod_ref[0]
    o_ref[...] = x_ref[...] + mod[5:6] * _rms(y, g_ref[3:4])


def _mixer_core(decay_logit, x, mod, g_norm, q, k, v, sg, ug, vsn, gr, gs, sf, sb, g_ret, w_sp, b_sp_t,
                w_ret, w_sg, w_out, *, rows_per_mod, seq_len, zero_init):
    t = x.shape[0]
    tm = TOKEN_TILE
    if rows_per_mod is None:
        mod_map = lambda i: (0, 0, 0)
    else:
        mod_map = lambda i: ((i * tm) // rows_per_mod, 0, 0)
    tok = lambda w: pl.BlockSpec((tm, w), lambda i: (i, 0))
    state_spec = pl.BlockSpec((tm // CHUNK, N_HEADS, DK, DV), lambda i: (i, 0, 0, 0))
    return pl.pallas_call(
        functools.partial(_mixer_core_kernel, chunks_per_seq=seq_len // CHUNK, zero_init=zero_init),
        grid=(t // tm,),
        in_specs=[_resident(decay_logit.shape), tok(D_MODEL), pl.BlockSpec((1, N_MOD, D_MODEL), mod_map),
                  _resident(g_norm.shape), tok(N_HEADS * DK), tok(N_HEADS * DK)] + [tok(D_MODEL)] * 6
                 + [state_spec, state_spec, _resident(g_ret.shape), _resident(w_sp.shape),
                    _resident(b_sp_t.shape), _resident(w_ret.shape), _resident(w_sg.shape),
                    _resident(w_out.shape)],
        out_specs=tok(D_MODEL),
        out_shape=jax.ShapeDtypeStruct((t, D_MODEL), F32),
        scratch_shapes=[pltpu.VMEM((tm, D_MODEL), BF16), pltpu.VMEM((tm, D_MODEL), BF16)],
        compiler_params=_params(),
        name="mixer_core",
    )(decay_logit, x, mod, g_norm, q, k, v, sg, ug, vsn, gr, gs, sf, sb, g_ret, w_sp, b_sp_t,
      w_ret, w_sg, w_out)


def _rope_tables(seq_len):
    rows = seq_len // GRID_W
    r = jnp.repeat(jnp.arange(rows), GRID_W).astype(F32)
    col = (jnp.arange(rows * GRID_W) % GRID_W).astype(F32)
    nf = DK // 4
    freqs = ROPE_THETA ** (-jnp.arange(nf, dtype=F32) / nf)
    ang = jnp.concatenate([r[:, None] * freqs, col[:, None] * freqs], axis=-1)
    cos, sin = jnp.cos(ang), jnp.sin(ang)
    return jnp.concatenate([cos, cos], axis=-1), jnp.concatenate([-sin, sin], axis=-1)


def _layer(x, mod, rope, s0, lw, *, rows_per_mod, seq_len):
    zero_init = s0 is None
    kw = dict(rows_per_mod=rows_per_mod)
    x = _ffn(x, mod, lw["g_norm"], lw["w_ffn1_in"], lw["w_ffn1_out"], mod0=0, gn0=0, **kw)
    q, k, v, sg, ug, vsn, gr, gs = _mixer_in(
        x, mod, lw["g_norm"], lw["w_in"], lw["g_sg"], lw["b_sg"], rope, seq_len=seq_len, **kw)
    st = _retention_states(lw["decay_logit"], k, v, s0, seq_len=seq_len)
    sf, sb = st[0], st[1]
    finals = (st[2], st[3]) if zero_init else None
    x = _mixer_core(lw["decay_logit"], x, mod, lw["g_norm"], q, k, v, sg, ug, vsn, gr, gs, sf, sb,
                    lw["g_ret"], lw["w_sp"], lw["b_sp_t"], lw["w_ret_br"], lw["w_sg_br"], lw["w_out"],
                    seq_len=seq_len, zero_init=zero_init, **kw)
    x = _ffn(x, mod, lw["g_norm"], lw["w_ffn2_in"], lw["w_ffn2_out"], mod0=6, gn0=4, **kw)
    return x, finals


def _ffn_in_layout(w):
    return w.astype(BF16).reshape(D_MODEL, 2, N_FF_CHUNKS, FF_CHUNK).transpose(1, 2, 0, 3)


def kernel(x_prompt, x_sample, state_ret_fwd, state_ret_bwd, c, c_ctx, w_ada, b_ada, g_norm, w_ffn1_in, w_ffn1_out, w_in, ret_decay_logit, g_ret, w_ret_br, g_sg, b_sg, w_sp, b_sp, w_sg_br, w_out, w_ffn2_in, w_ffn2_out):
    bp, lp, d = x_prompt.shape
    bs, ls, _ = x_sample.shape
    depth = w_ada.shape[0]
    yp = x_prompt.reshape(bp * lp, d)
    ys = x_sample.reshape(bs * ls, d)
    rope = _rope_tables(ls)
    c_rows = jnp.zeros((16, d), F32).at[:bs].set(c).at[bs].set(c_ctx)
    new_f, new_b = [], []
    for l in range(depth):
        lw = dict(
            g_norm=g_norm[l],
            w_ffn1_in=_ffn_in_layout(w_ffn1_in[l]),
            w_ffn1_out=w_ffn1_out[l].astype(BF16).reshape(N_FF_CHUNKS, FF_CHUNK, d),
            w_in=w_in[l].astype(BF16).reshape(d, N_IN_SEG, d).transpose(1, 0, 2),
            decay_logit=ret_decay_logit[l],
            g_ret=g_ret[l].reshape(1, d),
            w_ret_br=w_ret_br[l].astype(BF16),
            g_sg=g_sg[l].reshape(1, d),
            b_sg=b_sg[l].reshape(1, d),
            w_sp=w_sp[l].astype(BF16),
            b_sp_t=b_sp[l].T,
            w_sg_br=w_sg_br[l].astype(BF16),
            w_out=w_out[l].astype(BF16),
            w_ffn2_in=_ffn_in_layout(w_ffn2_in[l]),
            w_ffn2_out=w_ffn2_out[l].astype(BF16).reshape(N_FF_CHUNKS, FF_CHUNK, d),
        )
        mod = _modulation(c_rows, w_ada[l], b_ada[l]).reshape(16, N_MOD, d)
        mod_lat, mod_ctx = mod[:bs], mod[bs:bs + 1]
        yp, (sf, sb) = _layer(yp, mod_ctx, None, None, lw, rows_per_mod=None, seq_len=lp)
        new_f.append(sf)
        new_b.append(sb)
        s0 = (state_ret_fwd[:, l], state_ret_bwd[:, l])
        ys, _ = _layer(ys, mod_lat, rope, s0, lw, rows_per_mod=ls, seq_len=ls)
    new_state_ret_fwd = jnp.stack(new_f, axis=1).astype(x_prompt.dtype)
    new_state_ret_bwd = jnp.stack(new_b, axis=1).astype(x_prompt.dtype)
    return (yp.reshape(bp, lp, d), ys.reshape(bs, ls, d), new_state_ret_fwd, new_state_ret_bwd)
```

```python
import functools
import math

import jax
import jax.numpy as jnp
from jax import lax
from jax.experimental import pallas as pl
from jax.experimental.pallas import tpu as pltpu

D_MODEL = 1024
GRID_W = 64
CHUNK = 128
N_HEADS = 4
DK = D_MODEL // 8
DV = D_MODEL // 4
N_GROUPS = 4
GROUP_W = D_MODEL // N_GROUPS
D_FF = 2816
FF_CHUNK = 256
N_FF_CHUNKS = D_FF // FF_CHUNK
N_MOD = 9
N_IN_SEG = 7
ROPE_THETA = 10000.0
EPS = 1e-6
TOKEN_TILE = 512
VMEM_LIMIT_BYTES = 56 * 1024 * 1024

F32 = jnp.float32
BF16 = jnp.bfloat16


def _params(n_axes=1):
    return pltpu.CompilerParams(
        dimension_semantics=("parallel",) * n_axes, vmem_limit_bytes=VMEM_LIMIT_BYTES)


def _resident(shape):
    nd = len(shape)
    return pl.BlockSpec(shape, lambda *_: (0,) * nd, pipeline_mode=pl.Buffered(1))


def _rms(x, g):
    return x * lax.rsqrt(jnp.mean(x * x, axis=-1, keepdims=True) + EPS) * g


def _dot(a, b):
    return jnp.dot(a, b, preferred_element_type=F32)


def _mod_kernel(c_ref, w_ref, b_ref, o_ref):
    c = c_ref[...]
    a = (c * jax.nn.sigmoid(c)).astype(BF16)
    o_ref[...] = _dot(a, w_ref[...].astype(BF16)) + b_ref[...]


def _modulation(c_rows, w_ada, b_ada):
    n = w_ada.shape[1]
    tn = D_MODEL // 2
    return pl.pallas_call(
        _mod_kernel,
        grid=(n // tn,),
        in_specs=[pl.BlockSpec((16, D_MODEL), lambda j: (0, 0)),
                  pl.BlockSpec((D_MODEL, tn), lambda j: (0, j)),
                  pl.BlockSpec((1, tn), lambda j: (0, j))],
        out_specs=pl.BlockSpec((16, tn), lambda j: (0, j)),
        out_shape=jax.ShapeDtypeStruct((16, n), F32),
        compiler_params=_params(),
        name="adaln_mod",
    )(c_rows, w_ada, b_ada.reshape(1, n))


def _ffn_kernel(x_ref, mod_ref, g_ref, w1_ref, w2_ref, o_ref, *, mod0, gn0):
    x = x_ref[...]
    mod = mod_ref[0]
    g = g_ref[...]
    hh = (_rms(x, g[gn0:gn0 + 1]) * (1.0 + mod[mod0 + 1:mod0 + 2]) + mod[mod0:mod0 + 1]).astype(BF16)
    acc = None
    for j in range(N_FF_CHUNKS):
        cols = slice(j * FF_CHUNK, (j + 1) * FF_CHUNK)
        a = _dot(hh, w1_ref[:, cols])
        b = _dot(hh, w1_ref[:, D_FF + j * FF_CHUNK:D_FF + (j + 1) * FF_CHUNK])
        act = (a * jax.nn.sigmoid(a) * b).astype(BF16)
        y = _dot(act, w2_ref[cols, :])
        acc = y if acc is None else acc + y
    o_ref[...] = x + 0.5 * mod[mod0 + 2:mod0 + 3] * _rms(acc, g[gn0 + 1:gn0 + 2])


def _ffn(x, mod, g_norm, w1, w2, *, mod0, gn0, rows_per_mod):
    t = x.shape[0]
    tm = TOKEN_TILE
    if rows_per_mod is None:
        mod_map = lambda i: (0, 0, 0)
    else:
        mod_map = lambda i: ((i * tm) // rows_per_mod, 0, 0)
    return pl.pallas_call(
        functools.partial(_ffn_kernel, mod0=mod0, gn0=gn0),
        grid=(t // tm,),
        in_specs=[pl.BlockSpec((tm, D_MODEL), lambda i: (i, 0)),
                  pl.BlockSpec((1, N_MOD, D_MODEL), mod_map),
                  _resident(g_norm.shape),
                  _resident(w1.shape),
                  _resident(w2.shape)],
        out_specs=pl.BlockSpec((tm, D_MODEL), lambda i: (i, 0)),
        out_shape=jax.ShapeDtypeStruct((t, D_MODEL), F32),
        compiler_params=_params(),
        name="macaron_ffn",
    )(x, mod, g_norm, w1, w2)


def _rope_slab(x, cos, sin_signed):
    return x * cos + pltpu.roll(x, DK // 2, 1) * sin_signed


def _seg(w_ref, s):
    return w_ref[:, s * D_MODEL:(s + 1) * D_MODEL]


def _mixer_in_kernel(*refs, use_rope):
    if use_rope:
        (x_ref, mod_ref, g_ref, w_ref, gsg_ref, bsg_ref, cos_ref, sin_ref,
         q_ref, k_ref, v_ref, sg_ref, ug_ref, vsn_ref, gr_ref, gs_ref) = refs
    else:
        (x_ref, mod_ref, g_ref, w_ref, gsg_ref, bsg_ref,
         q_ref, k_ref, v_ref, sg_ref, ug_ref, vsn_ref, gr_ref, gs_ref) = refs
    x = x_ref[...]
    mod = mod_ref[0]
    g = g_ref[...]
    hh = (_rms(x, g[2:3]) * (1.0 + mod[4:5]) + mod[3:4]).astype(BF16)

    z = _dot(hh, _seg(w_ref, 0))
    q = z[:, :N_HEADS * DK] * (DK ** -0.5)
    k = z[:, N_HEADS * DK:]
    if use_rope:
        cos = cos_ref[...]
        sin = sin_ref[...]
        for h in range(N_HEADS):
            sl = slice(h * DK, (h + 1) * DK)
            q_ref[:, sl] = _rope_slab(q[:, sl], cos, sin).astype(BF16)
            k_ref[:, sl] = _rope_slab(k[:, sl], cos, sin).astype(BF16)
    else:
        q_ref[...] = q.astype(BF16)
        k_ref[...] = k.astype(BF16)

    v_ref[...] = _dot(hh, _seg(w_ref, 1)).astype(BF16)

    z = _dot(hh, _seg(w_ref, 2))
    sg_ref[...] = (z * jax.nn.sigmoid(z)).astype(BF16)

    ug_ref[...] = jax.nn.gelu(_dot(hh, _seg(w_ref, 3))).astype(BF16)

    vs = jax.nn.gelu(_dot(hh, _seg(w_ref, 4)))
    vc = vs - jnp.mean(vs, axis=-1, keepdims=True)
    vsn = vc * lax.rsqrt(jnp.mean(vc * vc, axis=-1, keepdims=True) + EPS)
    vsn_ref[...] = (vsn * gsg_ref[...] + bsg_ref[...]).astype(BF16)

    gr_ref[...] = jax.nn.sigmoid(_dot(hh, _seg(w_ref, 5))).astype(BF16)
    gs_ref[...] = jax.nn.sigmoid(_dot(hh, _seg(w_ref, 6))).astype(BF16)


def _mixer_in(x, mod, g_norm, w_in, g_sg, b_sg, rope, *, rows_per_mod, seq_len):
    t = x.shape[0]
    tm = TOKEN_TILE
    if rows_per_mod is None:
        mod_map = lambda i: (0, 0, 0)
    else:
        mod_map = lambda i: ((i * tm) // rows_per_mod, 0, 0)
    tok = lambda w: pl.BlockSpec((tm, w), lambda i: (i, 0))
    in_specs = [tok(D_MODEL), pl.BlockSpec((1, N_MOD, D_MODEL), mod_map), _resident(g_norm.shape),
                _resident(w_in.shape), _resident(g_sg.shape), _resident(b_sg.shape)]
    args = [x, mod, g_norm, w_in, g_sg, b_sg]
    if rope is not None:
        tiles_per_seq = seq_len // tm
        rope_spec = pl.BlockSpec((tm, DK), lambda i: (i % tiles_per_seq, 0))
        in_specs += [rope_spec, rope_spec]
        args += list(rope)
    widths = (N_HEADS * DK, N_HEADS * DK) + (D_MODEL,) * 6
    return pl.pallas_call(
        functools.partial(_mixer_in_kernel, use_rope=rope is not None),
        grid=(t // tm,),
        in_specs=in_specs,
        out_specs=[tok(w) for w in widths],
        out_shape=[jax.ShapeDtypeStruct((t, w), BF16) for w in widths],
        compiler_params=_params(),
        name="mixer_in",
    )(*args)


def _log_decay(logit_ref, d, h):
    x = logit_ref[d:d + 1, h:h + 1]
    return jnp.minimum(x, 0.0) - jnp.log1p(jnp.exp(-jnp.abs(x)))


def _state_kernel(*refs, n_chunks, zero_init):
    if zero_init:
        logit_ref, k_ref, v_ref, sf_ref, sb_ref, finf_ref, finb_ref = refs
    else:
        logit_ref, k_ref, v_ref, s0f_ref, s0b_ref, sf_ref, sb_ref = refs
    row = lax.broadcasted_iota(jnp.int32, (CHUNK, DK), 0).astype(F32)
    for h in range(N_HEADS):
        lgf = _log_decay(logit_ref, 0, h)
        lgb = _log_decay(logit_ref, 1, h)
        kdec_f = jnp.exp(lgf * (CHUNK - 1.0 - row))
        kdec_b = jnp.exp(lgb * row)
        gc_f = jnp.exp(lgf * CHUNK)
        gc_b = jnp.exp(lgb * CHUNK)

        def kv(c, kdec):
            rows = slice(c * CHUNK, (c + 1) * CHUNK)
            kd = (k_ref[rows, h * DK:(h + 1) * DK].astype(F32) * kdec).astype(BF16)
            vc = v_ref[rows, h * DV:(h + 1) * DV]
            return lax.dot_general(kd, vc, (((0,), (0,)), ((), ())), preferred_element_type=F32)

        s = jnp.zeros((DK, DV), F32) if zero_init else s0f_ref[0, h]
        for c in range(n_chunks):
            sf_ref[c, h] = s.astype(BF16)
            s = s * gc_f + kv(c, kdec_f)
        if zero_init:
            finf_ref[0, h] = s

        s = jnp.zeros((DK, DV), F32) if zero_init else s0b_ref[0, h]
        for c in range(n_chunks - 1, -1, -1):
            sb_ref[c, h] = s.astype(BF16)
            s = s * gc_b + kv(c, kdec_b)
        if zero_init:
            finb_ref[0, h] = s


def _retention_states(decay_logit, k, v, s0, *, seq_len):
    t = k.shape[0]
    n_seq = t // seq_len
    n_chunks = seq_len // CHUNK
    zero_init = s0 is None
    state_spec = pl.BlockSpec((n_chunks, N_HEADS, DK, DV), lambda s: (s, 0, 0, 0))
    per_seq = pl.BlockSpec((1, N_HEADS, DK, DV), lambda s: (s, 0, 0, 0))
    in_specs = [_resident(decay_logit.shape),
                pl.BlockSpec((seq_len, N_HEADS * DK), lambda s: (s, 0)),
                pl.BlockSpec((seq_len, D_MODEL), lambda s: (s, 0))]
    args = [decay_logit, k, v]
    states = jax.ShapeDtypeStruct((t // CHUNK, N_HEADS, DK, DV), BF16)
    out_specs = [state_spec, state_spec]
    out_shape = [states, states]
    if zero_init:
        final = jax.ShapeDtypeStruct((n_seq, N_HEADS, DK, DV), F32)
        out_specs += [per_seq, per_seq]
        out_shape += [final, final]
    else:
        in_specs += [per_seq, per_seq]
        args += list(s0)
    return pl.pallas_call(
        functools.partial(_state_kernel, n_chunks=n_chunks, zero_init=zero_init),
        grid=(n_seq,),
        in_specs=in_specs,
        out_specs=out_specs,
        out_shape=out_shape,
        compiler_params=_params(),
        name="retention_states",
    )(*args)


def _mixer_core_kernel(logit_ref, x_ref, mod_ref, g_ref, q_ref, k_ref, v_ref, sg_ref, ug_ref, vsn_ref,
                       gr_ref, gs_ref, sf_ref, sb_ref, gret_ref, wsp_ref, bsp_ref, wret_ref, wsg_ref,
                       wout_ref, o_ref, rg_ref, sgin_ref, *, chunks_per_seq, zero_init):
    n_tile_chunks = x_ref.shape[0] // CHUNK
    row = lax.broadcasted_iota(jnp.int32, (CHUNK, CHUNK), 0).astype(F32)
    col = lax.broadcasted_iota(jnp.int32, (CHUNK, CHUNK), 1).astype(F32)
    diff = row - col

    for h in range(N_HEADS):
        lgf = _log_decay(logit_ref, 0, h)
        lgb = _log_decay(logit_ref, 1, h)
        dmask = (jnp.where(diff >= 0, jnp.exp(lgf * jnp.maximum(diff, 0.0)), 0.0)
                 + jnp.where(diff <= 0, jnp.exp(lgb * jnp.maximum(-diff, 0.0)), 0.0))
        qdec_f = jnp.exp(lgf * (row + 1.0))
        qdec_b = jnp.exp(lgb * (CHUNK - row))
        gret = gret_ref[:, h * DV:(h + 1) * DV]
        for c in range(n_tile_chunks):
            rows = slice(c * CHUNK, (c + 1) * CHUNK)
            qc = q_ref[rows, h * DK:(h + 1) * DK]
            kc = k_ref[rows, h * DK:(h + 1) * DK]
            vc = v_ref[rows, h * DV:(h + 1) * DV]
            scores = lax.dot_general(qc, kc, (((1,), (1,)), ((), ())), preferred_element_type=F32)
            o = _dot((scores * dmask).astype(BF16), vc)
            qf = qc.astype(F32)
            seq_pos = c % chunks_per_seq
            if not (zero_init and seq_pos == 0):
                o = o + _dot((qf * qdec_f).astype(BF16), sf_ref[c, h])
            if not (zero_init and seq_pos == chunks_per_seq - 1):
                o = o + _dot((qf * qdec_b).astype(BF16), sb_ref[c, h])
            oc = o - jnp.mean(o, axis=-1, keepdims=True)
            rn = oc * lax.rsqrt(jnp.mean(oc * oc, axis=-1, keepdims=True) + EPS)
            sg = sg_ref[rows, h * DV:(h + 1) * DV].astype(F32)
            rg_ref[rows, h * DV:(h + 1) * DV] = (sg * (rn * gret)).astype(BF16)

    for gi in range(N_GROUPS):
        wsp = wsp_ref[gi]
        bias = bsp_ref[:, gi:gi + 1]
        cols = slice(gi * GROUP_W, (gi + 1) * GROUP_W)
        for c in range(n_tile_chunks):
            rows = slice(c * CHUNK, (c + 1) * CHUNK)
            sp = _dot(wsp, vsn_ref[rows, cols]) + bias
            sgin_ref[rows, cols] = (ug_ref[rows, cols].astype(F32) * sp).astype(BF16)

    ret_out = _dot(rg_ref[...], wret_ref[...])
    sg_out = _dot(sgin_ref[...], wsg_ref[...])
    merged = gr_ref[...].astype(F32) * ret_out + gs_ref[...].astype(F32) * sg_out
    y = _dot(merged.astype(BF16), wout_ref[...])
    mod = mod_ref[0]
    o_ref[...] = x_ref[...] + mod[5:6] * _rms(y, g_ref[3:4])


def _mixer_core(decay_logit, x, mod, g_norm, q, k, v, sg, ug, vsn, gr, gs, sf, sb, g_ret, w_sp, b_sp_t,
                w_ret, w_sg, w_out, *, rows_per_mod, seq_len, zero_init):
    t = x.shape[0]
    tm = TOKEN_TILE
    if rows_per_mod is None:
        mod_map = lambda i: (0, 0, 0)
    else:
        mod_map = lambda i: ((i * tm) // rows_per_mod, 0, 0)
    tok = lambda w: pl.BlockSpec((tm, w), lambda i: (i, 0))
    state_spec = pl.BlockSpec((tm // CHUNK, N_HEADS, DK, DV), lambda i: (i, 0, 0, 0))
    return pl.pallas_call(
        functools.partial(_mixer_core_kernel, chunks_per_seq=seq_len // CHUNK, zero_init=zero_init),
        grid=(t // tm,),
        in_specs=[_resident(decay_logit.shape), tok(D_MODEL), pl.BlockSpec((1, N_MOD, D_MODEL), mod_map),
                  _resident(g_norm.shape), tok(N_HEADS * DK), tok(N_HEADS * DK)] + [tok(D_MODEL)] * 6
                 + [state_spec, state_spec, _resident(g_ret.shape), _resident(w_sp.shape),
                    _resident(b_sp_t.shape), _resident(w_ret.shape), _resident(w_sg.shape),
                    _resident(w_out.shape)],
        out_specs=tok(D_MODEL),
        out_shape=jax.ShapeDtypeStruct((t, D_MODEL), F32),
        scratch_shapes=[pltpu.VMEM((tm, D_MODEL), BF16), pltpu.VMEM((tm, D_MODEL), BF16)],
        compiler_params=_params(),
        name="mixer_core",
    )(decay_logit, x, mod, g_norm, q, k, v, sg, ug, vsn, gr, gs, sf, sb, g_ret, w_sp, b_sp_t,
      w_ret, w_sg, w_out)


def _rope_tables(seq_len):
    rows = seq_len // GRID_W
    r = jnp.repeat(jnp.arange(rows), GRID_W).astype(F32)
    col = (jnp.arange(rows * GRID_W) % GRID_W).astype(F32)
    nf = DK // 4
    freqs = ROPE_THETA ** (-jnp.arange(nf, dtype=F32) / nf)
    ang = jnp.concatenate([r[:, None] * freqs, col[:, None] * freqs], axis=-1)
    cos, sin = jnp.cos(ang), jnp.sin(ang)
    return jnp.concatenate([cos, cos], axis=-1), jnp.concatenate([-sin, sin], axis=-1)


def _layer(x, mod, rope, s0, lw, *, rows_per_mod, seq_len):
    zero_init = s0 is None
    kw = dict(rows_per_mod=rows_per_mod)
    x = _ffn(x, mod, lw["g_norm"], lw["w_ffn1_in"], lw["w_ffn1_out"], mod0=0, gn0=0, **kw)
    q, k, v, sg, ug, vsn, gr, gs = _mixer_in(
        x, mod, lw["g_norm"], lw["w_in"], lw["g_sg"], lw["b_sg"], rope, seq_len=seq_len, **kw)
    st = _retention_states(lw["decay_logit"], k, v, s0, seq_len=seq_len)
    sf, sb = st[0], st[1]
    finals = (st[2], st[3]) if zero_init else None
    x = _mixer_core(lw["decay_logit"], x, mod, lw["g_norm"], q, k, v, sg, ug, vsn, gr, gs, sf, sb,
                    lw["g_ret"], lw["w_sp"], lw["b_sp_t"], lw["w_ret_br"], lw["w_sg_br"], lw["w_out"],
                    seq_len=seq_len, zero_init=zero_init, **kw)
    x = _ffn(x, mod, lw["g_norm"], lw["w_ffn2_in"], lw["w_ffn2_out"], mod0=6, gn0=4, **kw)
    return x, finals


def kernel(x_prompt, x_sample, state_ret_fwd, state_ret_bwd, c, c_ctx, w_ada, b_ada, g_norm, w_ffn1_in, w_ffn1_out, w_in, ret_decay_logit, g_ret, w_ret_br, g_sg, b_sg, w_sp, b_sp, w_sg_br, w_out, w_ffn2_in, w_ffn2_out):
    bp, lp, d = x_prompt.shape
    bs, ls, _ = x_sample.shape
    depth = w_ada.shape[0]
    yp = x_prompt.reshape(bp * lp, d)
    ys = x_sample.reshape(bs * ls, d)
    rope = _rope_tables(ls)
    c_rows = jnp.zeros((16, d), F32).at[:bs].set(c).at[bs].set(c_ctx)
    new_f, new_b = [], []
    for l in range(depth):
        lw = dict(
            g_norm=g_norm[l],
            w_ffn1_in=w_ffn1_in[l].astype(BF16),
            w_ffn1_out=w_ffn1_out[l].astype(BF16),
            w_in=w_in[l].astype(BF16),
            decay_logit=ret_decay_logit[l],
            g_ret=g_ret[l].reshape(1, d),
            w_ret_br=w_ret_br[l].astype(BF16),
            g_sg=g_sg[l].reshape(1, d),
            b_sg=b_sg[l].reshape(1, d),
            w_sp=w_sp[l].astype(BF16),
            b_sp_t=b_sp[l].T,
            w_sg_br=w_sg_br[l].astype(BF16),
            w_out=w_out[l].astype(BF16),
            w_ffn2_in=w_ffn2_in[l].astype(BF16),
            w_ffn2_out=w_ffn2_out[l].astype(BF16),
        )
        mod = _modulation(c_rows, w_ada[l], b_ada[l]).reshape(16, N_MOD, d)
        mod_lat, mod_ctx = mod[:bs], mod[bs:bs + 1]
        yp, (sf, sb) = _layer(yp, mod_ctx, None, None, lw, rows_per_mod=None, seq_len=lp)
        new_f.append(sf)
        new_b.append(sb)
        s0 = (state_ret_fwd[:, l], state_ret_bwd[:, l])
        ys, _ = _layer(ys, mod_lat, rope, s0, lw, rows_per_mod=ls, seq_len=ls)
    new_state_ret_fwd = jnp.stack(new_f, axis=1).astype(x_prompt.dtype)
    new_state_ret_bwd = jnp.stack(new_b, axis=1).astype(x_prompt.dtype)
    return (yp.reshape(bp, lp, d), ys.reshape(bs, ls, d), new_state_ret_fwd, new_state_ret_bwd)
```

```python
import functools

import jax
import jax.numpy as jnp
from jax import lax
from jax.experimental import pallas as pl
from jax.experimental.pallas import tpu as pltpu

D_MODEL = 1024
GRID_W = 64
CHUNK = 128
N_HEADS = 4
DK = D_MODEL // 8
DV = D_MODEL // 4
N_GROUPS = 4
GROUP_W = D_MODEL // N_GROUPS
D_FF = 2816
FF_CHUNK = 256
N_FF_CHUNKS = D_FF // FF_CHUNK
N_MOD = 9
MOD_ROWS = 16
N_IN_SEG = 7
ROPE_THETA = 10000.0
EPS = 1e-6
TOKEN_TILE = 512
STAGE_COLS = 512
STAGE_ROWS = 256
VMEM_LIMIT_BYTES = 56 * 1024 * 1024

F32 = jnp.float32
BF16 = jnp.bfloat16


def _params(semantics):
    return pltpu.CompilerParams(dimension_semantics=semantics, vmem_limit_bytes=VMEM_LIMIT_BYTES)


def _resident(shape):
    nd = len(shape)
    return pl.BlockSpec(shape, lambda *_: (0,) * nd, pipeline_mode=pl.Buffered(1))


_HBM = pl.BlockSpec(memory_space=pl.ANY)


def _rms(x, g):
    return x * lax.rsqrt(jnp.mean(x * x, axis=-1, keepdims=True) + EPS) * g


def _dot(a, b):
    return jnp.dot(a, b, preferred_element_type=F32)


def _col_blocks(n_cols):
    return [(slice(None), slice(c, c + STAGE_COLS)) for c in range(0, n_cols, STAGE_COLS)]


def _row_blocks(n_rows):
    return [(slice(r, r + STAGE_ROWS), slice(None)) for r in range(0, n_rows, STAGE_ROWS)]


def _stream_cast(jobs, stage, sem):
    def copy(b):
        w_hbm, _, idx = jobs[b]
        return pltpu.make_async_copy(w_hbm.at[idx], stage.at[b % 2], sem.at[b % 2])

    n = len(jobs)
    copy(0).start()
    if n > 1:
        copy(1).start()
    for b in range(n):
        copy(b).wait()
        _, w_bf, idx = jobs[b]
        w_bf[idx] = stage[b % 2].astype(BF16)
        if b + 2 < n:
            copy(b + 2).start()


class _Groups:
    def __init__(self, n_ctx_tokens, n_lat_tokens, lat_seq_len):
        self.tm = TOKEN_TILE
        self.ctx_tiles = n_ctx_tokens // self.tm
        self.lat_tiles = n_lat_tokens // self.tm
        self.n_tiles = self.ctx_tiles + self.lat_tiles
        self.tiles_per_lat_seq = lat_seq_len // self.tm
        self.n_lat_seq = n_lat_tokens // lat_seq_len

    def tok(self, width):
        return pl.BlockSpec((self.tm, width), lambda i: (i, 0))

    def ctx_tok(self, width):
        last = self.ctx_tiles - 1
        return pl.BlockSpec((self.tm, width), lambda i: (jnp.minimum(i, last), 0))

    def lat_tok(self, width):
        first = self.ctx_tiles
        return pl.BlockSpec((self.tm, width), lambda i: (jnp.maximum(i - first, 0), 0))

    def mod(self):
        first, per, ctx_row = self.ctx_tiles, self.tiles_per_lat_seq, self.n_lat_seq
        return pl.BlockSpec((1, N_MOD, D_MODEL),
                            lambda i: (jnp.where(i < first, ctx_row, (i - first) // per), 0, 0))

    def rope(self):
        first, per = self.ctx_tiles, self.tiles_per_lat_seq
        return pl.BlockSpec((self.tm, DK), lambda i: (jnp.where(i < first, per, (i - first) % per), 0))


def _mod_kernel(c_ref, w_ref, b_ref, o_ref):
    c = c_ref[...]
    a = (c * jax.nn.sigmoid(c)).astype(BF16)
    o_ref[...] = _dot(a, w_ref[...].astype(BF16)) + b_ref[...]


def _modulation(c_rows, w_ada, b_ada):
    n = w_ada.shape[1]
    tn = D_MODEL // 2
    return pl.pallas_call(
        _mod_kernel,
        grid=(n // tn,),
        in_specs=[pl.BlockSpec((MOD_ROWS, D_MODEL), lambda j: (0, 0)),
                  pl.BlockSpec((D_MODEL, tn), lambda j: (0, j)),
                  pl.BlockSpec((1, tn), lambda j: (0, j))],
        out_specs=pl.BlockSpec((MOD_ROWS, tn), lambda j: (0, j)),
        out_shape=jax.ShapeDtypeStruct((MOD_ROWS, n), F32),
        compiler_params=_params(("parallel",)),
        name="adaln_mod",
    )(c_rows, w_ada, b_ada.reshape(1, n))


def _ffn_kernel(*refs, mod0, gn0, ctx_tiles, split_in, split_out):
    refs = list(refs)
    xs = [refs.pop(0) for _ in range(2 if split_in else 1)]
    mod_ref, g_ref, w1_hbm, w2_hbm = [refs.pop(0) for _ in range(4)]
    outs = [refs.pop(0) for _ in range(2 if split_out else 1)]
    w1_ref, w2_ref, stage1, stage2, sem1, sem2 = refs
    i = pl.program_id(0)

    @pl.when(i == 0)
    def _():
        _stream_cast([(w1_hbm, w1_ref, idx) for idx in _col_blocks(2 * D_FF)], stage1, sem1)
        _stream_cast([(w2_hbm, w2_ref, idx) for idx in _row_blocks(D_FF)], stage2, sem2)

    if split_in:
        x = jnp.where(i < ctx_tiles, xs[0][...], xs[1][...])
    else:
        x = xs[0][...]
    mod = mod_ref[0]
    g = g_ref[...]
    hh = (_rms(x, g[gn0:gn0 + 1]) * (1.0 + mod[mod0 + 1:mod0 + 2]) + mod[mod0:mod0 + 1]).astype(BF16)
    acc = None
    for j in range(N_FF_CHUNKS):
        cols = slice(j * FF_CHUNK, (j + 1) * FF_CHUNK)
        a = _dot(hh, w1_ref[:, cols])
        b = _dot(hh, w1_ref[:, D_FF + j * FF_CHUNK:D_FF + (j + 1) * FF_CHUNK])
        act = (a * jax.nn.sigmoid(a) * b).astype(BF16)
        y = _dot(act, w2_ref[cols, :])
        acc = y if acc is None else acc + y
    res = x + 0.5 * mod[mod0 + 2:mod0 + 3] * _rms(acc, g[gn0 + 1:gn0 + 2])
    if split_out:
        @pl.when(i < ctx_tiles)
        def _():
            outs[0][...] = res

        @pl.when(i >= ctx_tiles)
        def _():
            outs[1][...] = res
    else:
        outs[0][...] = res


def _ffn(xs, mod, g_norm, w1, w2, grp, *, mod0, gn0, split_out):
    split_in = len(xs) == 2
    tm = grp.tm
    in_specs = ([grp.ctx_tok(D_MODEL), grp.lat_tok(D_MODEL)] if split_in else [grp.tok(D_MODEL)])
    in_specs += [grp.mod(), _resident(g_norm.shape), _HBM, _HBM]
    if split_out:
        out_specs = [grp.ctx_tok(D_MODEL), grp.lat_tok(D_MODEL)]
        out_shape = [jax.ShapeDtypeStruct((grp.ctx_tiles * tm, D_MODEL), F32),
                     jax.ShapeDtypeStruct((grp.lat_tiles * tm, D_MODEL), F32)]
    else:
        out_specs = [grp.tok(D_MODEL)]
        out_shape = [jax.ShapeDtypeStruct((grp.n_tiles * tm, D_MODEL), F32)]
    return pl.pallas_call(
        functools.partial(_ffn_kernel, mod0=mod0, gn0=gn0, ctx_tiles=grp.ctx_tiles,
                          split_in=split_in, split_out=split_out),
        grid=(grp.n_tiles,),
        in_specs=in_specs,
        out_specs=out_specs,
        out_shape=out_shape,
        scratch_shapes=[pltpu.VMEM(w1.shape, BF16), pltpu.VMEM(w2.shape, BF16),
                        pltpu.VMEM((2, D_MODEL, STAGE_COLS), F32), pltpu.VMEM((2, STAGE_ROWS, D_MODEL), F32),
                        pltpu.SemaphoreType.DMA((2,)), pltpu.SemaphoreType.DMA((2,))],
        compiler_params=_params(("arbitrary",)),
        name="macaron_ffn",
    )(*xs, mod, g_norm, w1, w2)


def _rope_slab(x, cos, sin_signed):
    return x * cos + pltpu.roll(x, DK // 2, 1) * sin_signed


def _seg(w_ref, s):
    return w_ref[:, s * D_MODEL:(s + 1) * D_MODEL]


def _mixer_in_kernel(x_ref, mod_ref, g_ref, w_hbm, gsg_ref, bsg_ref, cos_ref, sin_ref,
                     q_ref, k_ref, v_ref, sg_ref, ug_ref, vsn_ref, gr_ref, gs_ref,
                     w_ref, stage, sem):
    @pl.when(pl.program_id(0) == 0)
    def _():
        _stream_cast([(w_hbm, w_ref, idx) for idx in _col_blocks(N_IN_SEG * D_MODEL)], stage, sem)

    x = x_ref[...]
    mod = mod_ref[0]
    g = g_ref[...]
    hh = (_rms(x, g[2:3]) * (1.0 + mod[4:5]) + mod[3:4]).astype(BF16)

    z = _dot(hh, _seg(w_ref, 0))
    q = z[:, :N_HEADS * DK] * (DK ** -0.5)
    k = z[:, N_HEADS * DK:]
    cos = cos_ref[...]
    sin = sin_ref[...]
    for h in range(N_HEADS):
        sl = slice(h * DK, (h + 1) * DK)
        q_ref[:, sl] = _rope_slab(q[:, sl], cos, sin).astype(BF16)
        k_ref[:, sl] = _rope_slab(k[:, sl], cos, sin).astype(BF16)

    v_ref[...] = _dot(hh, _seg(w_ref, 1)).astype(BF16)

    z = _dot(hh, _seg(w_ref, 2))
    sg_ref[...] = (z * jax.nn.sigmoid(z)).astype(BF16)

    ug_ref[...] = jax.nn.gelu(_dot(hh, _seg(w_ref, 3))).astype(BF16)

    vs = jax.nn.gelu(_dot(hh, _seg(w_ref, 4)))
    vc = vs - jnp.mean(vs, axis=-1, keepdims=True)
    vsn = vc * lax.rsqrt(jnp.mean(vc * vc, axis=-1, keepdims=True) + EPS)
    vsn_ref[...] = (vsn * gsg_ref[...] + bsg_ref[...]).astype(BF16)

    gr_ref[...] = jax.nn.sigmoid(_dot(hh, _seg(w_ref, 5))).astype(BF16)
    gs_ref[...] = jax.nn.sigmoid(_dot(hh, _seg(w_ref, 6))).astype(BF16)


def _mixer_in(x, mod, g_norm, w_in, g_sg, b_sg, rope, grp):
    t = x.shape[0]
    widths = (N_HEADS * DK, N_HEADS * DK) + (D_MODEL,) * 6
    return pl.pallas_call(
        _mixer_in_kernel,
        grid=(grp.n_tiles,),
        in_specs=[grp.tok(D_MODEL), grp.mod(), _resident(g_norm.shape), _HBM,
                  _resident(g_sg.shape), _resident(b_sg.shape), grp.rope(), grp.rope()],
        out_specs=[grp.tok(w) for w in widths],
        out_shape=[jax.ShapeDtypeStruct((t, w), BF16) for w in widths],
        scratch_shapes=[pltpu.VMEM(w_in.shape, BF16), pltpu.VMEM((2, D_MODEL, STAGE_COLS), F32),
                        pltpu.SemaphoreType.DMA((2,))],
        compiler_params=_params(("arbitrary",)),
        name="mixer_in",
    )(x, mod, g_norm, w_in, g_sg, b_sg, *rope)


def _log_decay(logit_ref, d, h):
    x = logit_ref[d:d + 1, h:h + 1]
    return jnp.minimum(x, 0.0) - jnp.log1p(jnp.exp(-jnp.abs(x)))


def _state_kernel(*refs, n_chunks, zero_init):
    if zero_init:
        logit_ref, k_ref, v_ref, st_ref, finf_ref, finb_ref = refs
    else:
        logit_ref, k_ref, v_ref, s0f_ref, s0b_ref, st_ref = refs
    row = lax.broadcasted_iota(jnp.int32, (CHUNK, DK), 0).astype(F32)
    for h in range(N_HEADS):
        lgf = _log_decay(logit_ref, 0, h)
        lgb = _log_decay(logit_ref, 1, h)
        kdec_f = jnp.exp(lgf * (CHUNK - 1.0 - row))
        kdec_b = jnp.exp(lgb * row)
        gc_f = jnp.exp(lgf * CHUNK)
        gc_b = jnp.exp(lgb * CHUNK)

        def kv(c, kdec):
            rows = slice(c * CHUNK, (c + 1) * CHUNK)
            kd = (k_ref[rows, h * DK:(h + 1) * DK].astype(F32) * kdec).astype(BF16)
            vc = v_ref[rows, h * DV:(h + 1) * DV]
            return lax.dot_general(kd, vc, (((0,), (0,)), ((), ())), preferred_element_type=F32)

        s = jnp.zeros((DK, DV), F32) if zero_init else s0f_ref[0, h]
        for c in range(n_chunks):
            st_ref[c, h, :DK, :] = s.astype(BF16)
            s = s * gc_f + kv(c, kdec_f)
        if zero_init:
            finf_ref[0, h] = s

        s = jnp.zeros((DK, DV), F32) if zero_init else s0b_ref[0, h]
        for c in range(n_chunks - 1, -1, -1):
            st_ref[c, h, DK:, :] = s.astype(BF16)
            s = s * gc_b + kv(c, kdec_b)
        if zero_init:
            finb_ref[0, h] = s


def _retention_states(decay_logit, k, v, *, n_seq, seq_len, first_token, s0=None):
    n_chunks = seq_len // CHUNK
    seq0 = first_token // seq_len
    zero_init = s0 is None
    state_spec = pl.BlockSpec((n_chunks, N_HEADS, 2 * DK, DV), lambda s: (s, 0, 0, 0))
    per_seq = pl.BlockSpec((1, N_HEADS, DK, DV), lambda s: (s, 0, 0, 0))
    in_specs = [_resident(decay_logit.shape),
                pl.BlockSpec((seq_len, N_HEADS * DK), lambda s: (seq0 + s, 0)),
                pl.BlockSpec((seq_len, D_MODEL), lambda s: (seq0 + s, 0))]
    args = [decay_logit, k, v]
    out_specs = [state_spec]
    out_shape = [jax.ShapeDtypeStruct((n_seq * n_chunks, N_HEADS, 2 * DK, DV), BF16)]
    if zero_init:
        final = jax.ShapeDtypeStruct((n_seq, N_HEADS, DK, DV), F32)
        out_specs += [per_seq, per_seq]
        out_shape += [final, final]
    else:
        in_specs += [per_seq, per_seq]
        args += [s0[0], s0[1]]
    return pl.pallas_call(
        functools.partial(_state_kernel, n_chunks=n_chunks, zero_init=zero_init),
        grid=(n_seq,),
        in_specs=in_specs,
        out_specs=out_specs,
        out_shape=out_shape,
        compiler_params=_params(("parallel",)),
        name="retention_states",
    )(*args)


def _mixer_core_kernel(logit_ref, x_ref, mod_ref, g_ref, q_ref, k_ref, v_ref, sg_ref, ug_ref, vsn_ref,
                       gr_ref, gs_ref, stc_ref, stl_ref, gret_ref, wsp_ref, bsp_ref, wret_hbm, wsg_hbm,
                       wout_hbm, o_ref, wret_ref, wsg_ref, wout_ref, stage, sem, rg_ref, sgin_ref, *,
                       ctx_tiles):
    is_ctx = pl.program_id(0) < ctx_tiles

    @pl.when(pl.program_id(0) == 0)
    def _():
        jobs = []
        for w_hbm, w_bf in ((wret_hbm, wret_ref), (wsg_hbm, wsg_ref), (wout_hbm, wout_ref)):
            jobs += [(w_hbm, w_bf, idx) for idx in _row_blocks(D_MODEL)]
        _stream_cast(jobs, stage, sem)

    n_tile_chunks = x_ref.shape[0] // CHUNK
    row = lax.broadcasted_iota(jnp.int32, (CHUNK, CHUNK), 0).astype(F32)
    col = lax.broadcasted_iota(jnp.int32, (CHUNK, CHUNK), 1).astype(F32)
    diff = row - col

    for h in range(N_HEADS):
        lgf = _log_decay(logit_ref, 0, h)
        lgb = _log_decay(logit_ref, 1, h)
        dmask = (jnp.where(diff >= 0, jnp.exp(lgf * jnp.maximum(diff, 0.0)), 0.0)
                 + jnp.where(diff <= 0, jnp.exp(lgb * jnp.maximum(-diff, 0.0)), 0.0))
        qdec_f = jnp.exp(lgf * (row + 1.0))
        qdec_b = jnp.exp(lgb * (CHUNK - row))
        gret = gret_ref[:, h * DV:(h + 1) * DV]
        for c in range(n_tile_chunks):
            rows = slice(c * CHUNK, (c + 1) * CHUNK)
            qc = q_ref[rows, h * DK:(h + 1) * DK]
            kc = k_ref[rows, h * DK:(h + 1) * DK]
            vc = v_ref[rows, h * DV:(h + 1) * DV]
            scores = lax.dot_general(qc, kc, (((1,), (1,)), ((), ())), preferred_element_type=F32)
            o = _dot((scores * dmask).astype(BF16), vc)
            qf = qc.astype(F32)
            qcat = jnp.concatenate([(qf * qdec_f).astype(BF16), (qf * qdec_b).astype(BF16)], axis=-1)
            o = o + _dot(qcat, jnp.where(is_ctx, stc_ref[c, h], stl_ref[c, h]))
            oc = o - jnp.mean(o, axis=-1, keepdims=True)
            rn = oc * lax.rsqrt(jnp.mean(oc * oc, axis=-1, keepdims=True) + EPS)
            sg = sg_ref[rows, h * DV:(h + 1) * DV].astype(F32)
            rg_ref[rows, h * DV:(h + 1) * DV] = (sg * (rn * gret)).astype(BF16)

    for gi in range(N_GROUPS):
        wsp = wsp_ref[gi].astype(BF16)
        bias = bsp_ref[:, gi:gi + 1]
        cols = slice(gi * GROUP_W, (gi + 1) * GROUP_W)
        for c in range(n_tile_chunks):
            rows = slice(c * CHUNK, (c + 1) * CHUNK)
            sp = _dot(wsp, vsn_ref[rows, cols]) + bias
            sgin_ref[rows, cols] = (ug_ref[rows, cols].astype(F32) * sp).astype(BF16)

    ret_out = _dot(rg_ref[...], wret_ref[...])
    sg_out = _dot(sgin_ref[...], wsg_ref[...])
    merged = gr_ref[...].astype(F32) * ret_out + gs_ref[...].astype(F32) * sg_out
    y = _dot(merged.astype(BF16), wout_ref[...])
    mod = mod_ref[0]
    o_ref[...] = x_ref[...] + mod[5:6] * _rms(y, g_ref[3:4])


def _mixer_core(decay_logit, x, mod, g_norm, q, k, v, sg, ug, vsn, gr, gs, st_ctx, st_lat, g_ret, w_sp,
                b_sp_t, w_ret, w_sg, w_out, grp):
    t = x.shape[0]
    tm = grp.tm
    tok = grp.tok
    state_block = (tm // CHUNK, N_HEADS, 2 * DK, DV)
    last_ctx, first_lat = grp.ctx_tiles - 1, grp.ctx_tiles
    st_ctx_spec = pl.BlockSpec(state_block, lambda i: (jnp.minimum(i, last_ctx), 0, 0, 0))
    st_lat_spec = pl.BlockSpec(state_block, lambda i: (jnp.maximum(i - first_lat, 0), 0, 0, 0))
    w_vmem = pltpu.VMEM((D_MODEL, D_MODEL), BF16)
    return pl.pallas_call(
        functools.partial(_mixer_core_kernel, ctx_tiles=grp.ctx_tiles),
        grid=(grp.n_tiles,),
        in_specs=[_resident(decay_logit.shape), tok(D_MODEL), grp.mod(), _resident(g_norm.shape),
                  tok(N_HEADS * DK), tok(N_HEADS * DK)] + [tok(D_MODEL)] * 6
                 + [st_ctx_spec, st_lat_spec, _resident(g_ret.shape), _resident(w_sp.shape),
                    _resident(b_sp_t.shape), _HBM, _HBM, _HBM],
        out_specs=tok(D_MODEL),
        out_shape=jax.ShapeDtypeStruct((t, D_MODEL), F32),
        scratch_shapes=[w_vmem, w_vmem, w_vmem, pltpu.VMEM((2, STAGE_ROWS, D_MODEL), F32),
                        pltpu.SemaphoreType.DMA((2,)),
                        pltpu.VMEM((tm, D_MODEL), BF16), pltpu.VMEM((tm, D_MODEL), BF16)],
        compiler_params=_params(("arbitrary",)),
        name="mixer_core",
    )(decay_logit, x, mod, g_norm, q, k, v, sg, ug, vsn, gr, gs, st_ctx, st_lat, g_ret, w_sp, b_sp_t,
      w_ret, w_sg, w_out)


def _rope_tables(seq_len, pad_rows):
    rows = seq_len // GRID_W
    r = jnp.repeat(jnp.arange(rows), GRID_W).astype(F32)
    col = (jnp.arange(rows * GRID_W) % GRID_W).astype(F32)
    nf = DK // 4
    freqs = ROPE_THETA ** (-jnp.arange(nf, dtype=F32) / nf)
    ang = jnp.concatenate([r[:, None] * freqs, col[:, None] * freqs], axis=-1)
    cos, sin = jnp.cos(ang), jnp.sin(ang)
    cos2 = jnp.concatenate([cos, cos], axis=-1)
    sin2 = jnp.concatenate([-sin, sin], axis=-1)
    return (jnp.concatenate([cos2, jnp.ones((pad_rows, DK), F32)], axis=0),
            jnp.concatenate([sin2, jnp.zeros((pad_rows, DK), F32)], axis=0))


def kernel(x_prompt, x_sample, state_ret_fwd, state_ret_bwd, c, c_ctx, w_ada, b_ada, g_norm, w_ffn1_in, w_ffn1_out, w_in, ret_decay_logit, g_ret, w_ret_br, g_sg, b_sg, w_sp, b_sp, w_sg_br, w_out, w_ffn2_in, w_ffn2_out):
    bp, lp, d = x_prompt.shape
    bs, ls, _ = x_sample.shape
    depth = w_ada.shape[0]
    n_ctx, n_lat = bp * lp, bs * ls
    grp = _Groups(n_ctx, n_lat, ls)
    xs = [x_prompt.reshape(n_ctx, d), x_sample.reshape(n_lat, d)]
    rope = _rope_tables(ls, grp.tm)
    c_rows = jnp.zeros((MOD_ROWS, d), F32).at[:bs].set(c).at[bs].set(c_ctx)
    new_f, new_b = [], []
    for l in range(depth):
        gn = g_norm[l]
        logit = ret_decay_logit[l]
        mod = _modulation(c_rows, w_ada[l], b_ada[l]).reshape(MOD_ROWS, N_MOD, d)
        (x,) = _ffn(xs, mod, gn, w_ffn1_in[l], w_ffn1_out[l], grp, mod0=0, gn0=0, split_out=False)
        q, k, v, sg, ug, vsn, gr, gs = _mixer_in(
            x, mod, gn, w_in[l], g_sg[l].reshape(1, d), b_sg[l].reshape(1, d), rope, grp)
        st_ctx, fin_f, fin_b = _retention_states(logit, k, v, n_seq=bp, seq_len=lp, first_token=0)
        (st_lat,) = _retention_states(logit, k, v, n_seq=bs, seq_len=ls, first_token=n_ctx,
                                      s0=(state_ret_fwd[:, l], state_ret_bwd[:, l]))
        new_f.append(fin_f)
        new_b.append(fin_b)
        x = _mixer_core(logit, x, mod, gn, q, k, v, sg, ug, vsn, gr, gs, st_ctx, st_lat,
                        g_ret[l].reshape(1, d),
                        w_sp[l], b_sp[l].T, w_ret_br[l], w_sg_br[l], w_out[l], grp)
        xs = _ffn([x], mod, gn, w_ffn2_in[l], w_ffn2_out[l], grp, mod0=6, gn0=4, split_out=True)
    new_state_ret_fwd = jnp.stack(new_f, axis=1).astype(x_prompt.dtype)
    new_state_ret_bwd = jnp.stack(new_b, axis=1).astype(x_prompt.dtype)
    return (xs[0].reshape(bp, lp, d), xs[1].reshape(bs, ls, d), new_state_ret_fwd, new_state_ret_bwd)
```
